```python
import jax, jax.numpy as jnp
from jax import lax
import numpy as np

D_MODEL = 2048
BATCH = 2
SEQ = 4096
DEPTH = 1

CHUNK = 64
EPS = 1e-6

GM_GROUPS = 8
GM_DIM = 1024
GM_GROUP_DIM = GM_DIM // GM_GROUPS
GM_BLOCK = 128

RET_HEADS = 8
RET_DK = 128
RET_DV = 256
RET_QK = RET_HEADS * RET_DK
RET_V = RET_HEADS * RET_DV
ROPE_BASE = 10000.0

IN_WIDTH = 2 * GM_DIM + 2 * RET_QK + 2 * RET_V
N_BRANCH = 2
N_MOD = 6

PEER_HEADS = 8
PEER_NKEYS = 128
PEER_NEXPERTS = PEER_NKEYS * PEER_NKEYS
PEER_DKEY = 256
PEER_DHALF = PEER_DKEY // 2
PEER_TOPK = 16
PEER_TOKEN_BLOCK = 128

kernel_name = "chunk_causal_gmlp_retention_peer_hybrid"


def rmsnorm(x, g):
    xf = x.astype(jnp.float32)
    y = xf * lax.rsqrt(jnp.mean(xf * xf, axis=-1, keepdims=True) + EPS)
    return (y * g.astype(jnp.float32)).astype(x.dtype)


def modulate(h, shift, scale):
    return h * (1.0 + scale[:, None, :]) + shift[:, None, :]


def rotary(t, positions):
    half = t.shape[-1] // 2
    freqs = ROPE_BASE ** (-jnp.arange(half, dtype=jnp.float32) / half)
    ang = positions.astype(jnp.float32)[:, :, None, None] * freqs
    cos, sin = jnp.cos(ang), jnp.sin(ang)
    t1, t2 = t[..., :half], t[..., half:]
    return jnp.concatenate([t1 * cos - t2 * sin, t1 * sin + t2 * cos], axis=-1)


def gmlp_mixer(za, v_gain, ws, bias):
    b_, s_ = za.shape[0], za.shape[1]
    za = jax.nn.gelu(za, approximate=False)
    u, v = jnp.split(za, 2, axis=-1)
    v = rmsnorm(v.reshape(b_, s_, GM_GROUPS, GM_GROUP_DIM),
                v_gain.reshape(GM_GROUPS, GM_GROUP_DIM))
    vb = v.reshape(b_, s_ // GM_BLOCK, GM_BLOCK, GM_GROUPS, GM_GROUP_DIM)
    cid = jnp.arange(GM_BLOCK) // CHUNK
    mask = cid[:, None] >= cid[None, :]
    wsm = jnp.where(mask[None], ws, jnp.zeros_like(ws))
    s = jnp.einsum('gij,bnjgd->bnigd', wsm, vb) + bias.T[None, None, :, :, None]
    return u * s.reshape(b_, s_, GM_DIM)


def retention(q, k, v, g, positions, gn_gain):
    b_, s_ = q.shape[0], q.shape[1]
    nc = s_ // CHUNK
    f32 = jnp.float32
    qf = rotary(q.reshape(b_, s_, RET_HEADS, RET_DK).astype(f32), positions)
    kf = rotary(k.reshape(b_, s_, RET_HEADS, RET_DK).astype(f32), positions) * (RET_DK ** -0.5)
    vf = v.reshape(b_, s_, RET_HEADS, RET_DV).astype(f32)

    def to_chunks(t):
        return t.reshape(b_, nc, CHUNK, RET_HEADS, t.shape[-1]).transpose(1, 0, 3, 2, 4)

    log_g = jnp.log1p(-jnp.exp2(-5.0 - jnp.arange(RET_HEADS, dtype=f32)))
    idx = jnp.arange(CHUNK, dtype=f32)
    d_intra = jnp.exp(log_g[:, None, None] * jnp.abs(idx[:, None] - idx[None, :]))
    q_dec = jnp.exp(log_g[:, None] * (idx + 1.0))
    k_dec = jnp.exp(log_g[:, None] * (CHUNK - 1.0 - idx))
    c_dec = jnp.exp(log_g * CHUNK)

    def step(state, inp):
        qc, kc, vc = inp
        att = jnp.einsum('bhid,bhjd->bhij', qc, kc) * d_intra
        o = jnp.einsum('bhij,bhjv->bhiv', att, vc) + jnp.einsum(
            'bhid,bhdv->bhiv', qc * q_dec[None, :, :, None], state)
        state = state * c_dec[None, :, None, None] + jnp.einsum(
            'bhjd,bhjv->bhdv', kc * k_dec[None, :, :, None], vc)
        return state, o

    s0 = jnp.zeros((b_, RET_HEADS, RET_DK, RET_DV), f32)
    _, o = lax.scan(step, s0, (to_chunks(qf), to_chunks(kf), to_chunks(vf)))
    o = o.transpose(1, 0, 3, 2, 4).reshape(b_, s_, RET_HEADS, RET_DV)
    mu = jnp.mean(o, axis=-1, keepdims=True)
    var = jnp.mean(jnp.square(o - mu), axis=-1, keepdims=True)
    on = (o - mu) * lax.rsqrt(var + EPS) * gn_gain.astype(f32).reshape(RET_HEADS, RET_DV)
    return jax.nn.silu(g) * on.reshape(b_, s_, RET_V).astype(g.dtype)


def peer(h, wq, subkeys, emb_u, emb_v):
    b_, s_, d_ = h.shape
    t_ = b_ * s_
    hf = h.reshape(t_, d_)
    q = (hf @ wq).reshape(t_, PEER_HEADS, 2, PEER_DHALF).astype(jnp.float32)
    sc = jnp.einsum('thpd,hpnd->thpn', q, subkeys.astype(jnp.float32))
    v1, i1 = lax.top_k(sc[:, :, 0], PEER_TOPK)
    v2, i2 = lax.top_k(sc[:, :, 1], PEER_TOPK)
    cand = (v1[..., :, None] + v2[..., None, :]).reshape(t_, PEER_HEADS, PEER_TOPK * PEER_TOPK)
    cidx = (i1[..., :, None] * PEER_NKEYS + i2[..., None, :]).reshape(t_, PEER_HEADS, PEER_TOPK * PEER_TOPK)
    top, sel = lax.top_k(cand, PEER_TOPK)
    eidx = jnp.take_along_axis(cidx, sel, axis=-1)
    gate = jax.nn.softmax(top, axis=-1).astype(h.dtype)

    nb = t_ // PEER_TOKEN_BLOCK
    xb = hf.reshape(nb, PEER_TOKEN_BLOCK, d_)
    ib = eidx.reshape(nb, PEER_TOKEN_BLOCK, PEER_HEADS, PEER_TOPK)
    gb = gate.reshape(nb, PEER_TOKEN_BLOCK, PEER_HEADS, PEER_TOPK)

    def block(args):
        xt, it, gt = args
        u = emb_u[it]
        a = jnp.einsum('td,thkd->thk', xt, u)
        w = jax.nn.gelu(a, approximate=False) * gt
        vv = emb_v[it]
        return jnp.einsum('thk,thkd->td', w, vv)

    y = lax.map(block, (xb, ib, gb))
    return y.reshape(b_, s_, d_)


def setup_inputs(seed: int = 0) -> dict:
    key = jax.random.key(seed)
    ks = jax.random.split(key, 24)
    n = jax.random.normal
    f32 = jnp.float32
    D = D_MODEL
    x = n(ks[0], (BATCH, SEQ, D), f32)
    c = n(ks[1], (BATCH, D), f32)
    offset = jax.random.randint(ks[2], (BATCH, 1), 0, 64, dtype=jnp.int32) * CHUNK
    positions = jnp.arange(SEQ, dtype=jnp.int32)[None, :] + offset
    return {
        "x": x,
        "c": c,
        "positions": positions,
        "w_ada": n(ks[3], (DEPTH, D, N_MOD * D), f32) * (0.5 * D ** -0.5),
        "b_ada": n(ks[4], (DEPTH, N_MOD * D), f32) * 0.02,
        "norm1_g": 1.0 + 0.02 * n(ks[5], (DEPTH, D), f32),
        "w_in": n(ks[6], (DEPTH, D, IN_WIDTH), f32) * D ** -0.5,
        "w_branch_gate": n(ks[7], (DEPTH, D, N_BRANCH * D), f32) * D ** -0.5,
        "b_branch_gate": n(ks[8], (DEPTH, N_BRANCH * D), f32) * 0.02,
        "gm_v_g": 1.0 + 0.02 * n(ks[9], (DEPTH, GM_DIM), f32),
        "gm_ws": n(ks[10], (DEPTH, GM_GROUPS, GM_BLOCK, GM_BLOCK), f32) * (0.5 * GM_BLOCK ** -0.5),
        "gm_b": 1.0 + 0.1 * n(ks[11], (DEPTH, GM_GROUPS, GM_BLOCK), f32),
        "ret_gn_g": 1.0 + 0.02 * n(ks[12], (DEPTH, RET_V), f32),
        "w_a_out": n(ks[13], (DEPTH, GM_DIM, D), f32) * GM_DIM ** -0.5,
        "w_b_out": n(ks[14], (DEPTH, RET_V, D), f32) * RET_V ** -0.5,
        "w_o": n(ks[15], (DEPTH, D, D), f32) * D ** -0.5,
        "norm2_g": 1.0 + 0.02 * n(ks[16], (DEPTH, D), f32),
        "peer_wq": n(ks[17], (DEPTH, D, PEER_HEADS * PEER_DKEY), f32) * D ** -0.5,
        "peer_subkeys": n(ks[18], (DEPTH, PEER_HEADS, 2, PEER_NKEYS, PEER_DHALF), f32) * PEER_DHALF ** -0.5,
        "peer_u": n(ks[19], (DEPTH, PEER_NEXPERTS, D), f32) * D ** -0.5,
        "peer_v": n(ks[20], (DEPTH, PEER_NEXPERTS, D), f32) * 0.5,
        "norm_f_g": 1.0 + 0.02 * n(ks[21], (D,), f32),
    }


def reference(x, c, positions, w_ada, b_ada, norm1_g, w_in, w_branch_gate, b_branch_gate,
              gm_v_g, gm_ws, gm_b, ret_gn_g, w_a_out, w_b_out, w_o, norm2_g,
              peer_wq, peer_subkeys, peer_u, peer_v, norm_f_g):
    for l in range(DEPTH):
        ada = jax.nn.silu(c) @ w_ada[l] + b_ada[l]
        sh1, sc1, ga1, sh2, sc2, ga2 = jnp.split(ada, N_MOD, axis=-1)

        h = modulate(rmsnorm(x, norm1_g[l]), sh1, sc1)
        z = h @ w_in[l]
        za, zq, zk, zv, zg = jnp.split(
            z, [2 * GM_DIM, 2 * GM_DIM + RET_QK, 2 * GM_DIM + 2 * RET_QK,
                2 * GM_DIM + 2 * RET_QK + RET_V], axis=-1)
        ya = gmlp_mixer(za, gm_v_g[l], gm_ws[l], gm_b[l]) @ w_a_out[l]
        yb = retention(zq, zk, zv, zg, positions, ret_gn_g[l]) @ w_b_out[l]
        gates = jax.nn.sigmoid(h @ w_branch_gate[l] + b_branch_gate[l])
        gate_a, gate_b = jnp.split(gates, N_BRANCH, axis=-1)
        mix = (gate_a * ya + gate_b * yb) @ w_o[l]
        x = x + ga1[:, None, :] * mix

        h2 = modulate(rmsnorm(x, norm2_g[l]), sh2, sc2)
        x = x + ga2[:, None, :] * peer(h2, peer_wq[l], peer_subkeys[l], peer_u[l], peer_v[l])
    return rmsnorm(x, norm_f_g)
```

```python
import functools

import jax
import jax.numpy as jnp
from jax import lax
from jax.experimental import pallas as pl
from jax.experimental.pallas import tpu as pltpu

F32 = jnp.float32
BF16 = jnp.bfloat16

EPS = 1e-6
CHUNK = 64
GM_GROUPS = 8
GM_DIM = 1024
GM_GROUP_DIM = GM_DIM // GM_GROUPS
GM_BLOCK = 128
RET_HEADS = 8
RET_DK = 128
RET_DV = 256
RET_QK = RET_HEADS * RET_DK
RET_V = RET_HEADS * RET_DV
ROPE_BASE = 10000.0
PEER_HEADS = 8
PEER_NKEYS = 128
PEER_DHALF = 128
PEER_TOPK = 16

LANES = 128
VMEM_LIMIT = 56 * 1024 * 1024
NEG_BIG = -3.0e38

ADA_TN = 1024
INPROJ_TM = 512
INPROJ_TN = 1024
MIX_ROWS = 256
OUTPROJ_TM = 256
PEERQ_TQ = 256
PEER_TT = 512
PEER_TE = 1024
FINAL_TM = 512


def _params(*sem):
    return pltpu.CompilerParams(dimension_semantics=sem, vmem_limit_bytes=VMEM_LIMIT)


def _gelu(x):
    return 0.5 * x * (1.0 + lax.erf(x * (0.5 ** 0.5)))


def _rms(x):
    return x * lax.rsqrt(jnp.mean(x * x, axis=-1, keepdims=True) + EPS)


def _dot(a, b, **kw):
    return jnp.dot(a, b, preferred_element_type=F32, **kw)


def _dot_nt(a, b, **kw):
    return lax.dot_general(a, b, (((1,), (1,)), ((), ())), preferred_element_type=F32, **kw)


def _ada_kernel(c_ref, w_ref, b_ref, o_ref):
    c = c_ref[...]
    s = c * jax.nn.sigmoid(c)
    o_ref[...] = _dot(s, w_ref[...], precision=lax.Precision.HIGHEST) + b_ref[...]


def _ada(c8, w, b):
    d, n = w.shape
    return pl.pallas_call(
        _ada_kernel,
        out_shape=jax.ShapeDtypeStruct((8, n), F32),
        grid=(n // ADA_TN,),
        in_specs=[
            pl.BlockSpec((8, d), lambda j: (0, 0)),
            pl.BlockSpec((d, ADA_TN), lambda j: (0, j)),
            pl.BlockSpec((1, ADA_TN), lambda j: (0, j)),
        ],
        out_specs=pl.BlockSpec((8, ADA_TN), lambda j: (0, j)),
        compiler_params=_params("arbitrary"),
        name="ada",
    )(c8, w, b)


def _inproj_kernel(x_ref, g_ref, sc_ref, sh_ref, w_ref, b_ref, o_ref, h_scr, *, n_gelu, n_raw, n_silu):
    j = pl.program_id(1)

    @pl.when(j == 0)
    def _():
        y = _rms(x_ref[...])
        h = (y * g_ref[...]) * (1.0 + sc_ref[...]) + sh_ref[...]
        h_scr[...] = h.astype(BF16)

    acc = _dot(h_scr[...], w_ref[...]) + b_ref[...]

    @pl.when(j < n_gelu)
    def _():
        o_ref[...] = _gelu(acc).astype(BF16)

    @pl.when(jnp.logical_and(j >= n_gelu, j < n_raw))
    def _():
        o_ref[...] = acc.astype(BF16)

    @pl.when(jnp.logical_and(j >= n_raw, j < n_silu))
    def _():
        o_ref[...] = (acc * jax.nn.sigmoid(acc)).astype(BF16)

    @pl.when(j >= n_silu)
    def _():
        o_ref[...] = jax.nn.sigmoid(acc).astype(BF16)


def _inproj(x2d, g, sc, sh, w, b, seq):
    t, d = x2d.shape
    n = w.shape[1]
    tm, tn = INPROJ_TM, INPROJ_TN
    per_b = seq // tm
    kern = functools.partial(
        _inproj_kernel,
        n_gelu=(2 * GM_DIM) // tn,
        n_raw=(2 * GM_DIM + 2 * RET_QK + RET_V) // tn,
        n_silu=(2 * GM_DIM + 2 * RET_QK + 2 * RET_V) // tn,
    )
    return pl.pallas_call(
        kern,
        out_shape=jax.ShapeDtypeStruct((t, n), BF16),
        grid=(t // tm, n // tn),
        in_specs=[
            pl.BlockSpec((tm, d), lambda i, j: (i, 0)),
            pl.BlockSpec((1, d), lambda i, j: (0, 0)),
            pl.BlockSpec((None, 1, d), lambda i, j: (i // per_b, 0, 0)),
            pl.BlockSpec((None, 1, d), lambda i, j: (i // per_b, 0, 0)),
            pl.BlockSpec((d, tn), lambda i, j: (0, j)),
            pl.BlockSpec((1, tn), lambda i, j: (0, j)),
        ],
        out_specs=pl.BlockSpec((tm, tn), lambda i, j: (i, j)),
        scratch_shapes=[pltpu.VMEM((tm, d), BF16)],
        compiler_params=_params("arbitrary", "arbitrary"),
        name="in_proj",
    )(x2d, g, sc, sh, w, b)


def _mixer_kernel(za_ref, q_ref, k_ref, v_ref, g_ref, pos_ref, freq_ref, sign_ref,
                  mask_ref, qdec_ref, kdec_ref, cdec_ref, gn_ref, vgain_ref, ws_ref, gb_ref,
                  ya_ref, yb_ref, state_ref, *, rows):
    @pl.when(pl.program_id(1) == 0)
    def _():
        state_ref[...] = jnp.zeros_like(state_ref)

    ang = pos_ref[...] * freq_ref[...]
    cos2 = jnp.cos(ang)
    sin2 = jnp.sin(ang) * sign_ref[...]
    for h in range(RET_HEADS):
        ks = slice(h * RET_DK, (h + 1) * RET_DK)
        vs = slice(h * RET_DV, (h + 1) * RET_DV)
        qh = q_ref[:, ks].astype(F32)
        kh = k_ref[:, ks].astype(F32)
        qr = qh * cos2 + pltpu.roll(qh, RET_DK // 2, 1) * sin2
        kr = (kh * cos2 + pltpu.roll(kh, RET_DK // 2, 1) * sin2) * (RET_DK ** -0.5)
        vh = v_ref[:, vs]
        att = _dot_nt(qr.astype(BF16), kr.astype(BF16)) * mask_ref[h]
        st = state_ref[h]
        o = _dot(att.astype(BF16), vh) + _dot((qr * qdec_ref[h]).astype(BF16), st.astype(BF16))
        kd = (kr * kdec_ref[h]).T.astype(BF16)
        state_ref[h] = st * cdec_ref[h] + _dot(kd, vh)
        mu = jnp.mean(o, axis=-1, keepdims=True)
        oc = o - mu
        var = jnp.mean(oc * oc, axis=-1, keepdims=True)
        on = oc * lax.rsqrt(var + EPS) * gn_ref[:, vs]
        yb_ref[:, vs] = (g_ref[:, vs].astype(F32) * on).astype(BF16)

    rc = lax.broadcasted_iota(jnp.int32, (GM_BLOCK, GM_BLOCK), 0) // CHUNK
    cc = lax.broadcasted_iota(jnp.int32, (GM_BLOCK, GM_BLOCK), 1) // CHUNK
    causal = rc >= cc
    for g in range(GM_GROUPS):
        us = slice(g * GM_GROUP_DIM, (g + 1) * GM_GROUP_DIM)
        vsl = slice(GM_DIM + g * GM_GROUP_DIM, GM_DIM + (g + 1) * GM_GROUP_DIM)
        vn = (_rms(za_ref[:, vsl].astype(F32)) * vgain_ref[:, us]).astype(BF16)
        w = jnp.where(causal, ws_ref[g], 0.0).astype(BF16)
        for nb in range(rows // GM_BLOCK):
            rs = slice(nb * GM_BLOCK, (nb + 1) * GM_BLOCK)
            s = _dot(w, vn[rs]) + gb_ref[g]
            ya_ref[rs, us] = (za_ref[rs, us].astype(F32) * s).astype(BF16)


def _mixer(z_all, pos, freq2, sign2, mask, qdec, kdec, cdec, gn, vgain, ws, gb, batch, seq):
    t = z_all.shape[0]
    r = MIX_ROWS
    per_b = seq // r
    row = lambda b, n: b * per_b + n
    za_w = 2 * GM_DIM
    q_blk = za_w // RET_QK
    v_blk = (za_w + 2 * RET_QK) // RET_V
    const2 = lambda b, n: (0, 0)
    const3 = lambda b, n: (0, 0, 0)
    return pl.pallas_call(
        functools.partial(_mixer_kernel, rows=r),
        out_shape=(jax.ShapeDtypeStruct((t, GM_DIM), BF16), jax.ShapeDtypeStruct((t, RET_V), BF16)),
        grid=(batch, per_b),
        in_specs=[
            pl.BlockSpec((r, za_w), lambda b, n: (row(b, n), 0)),
            pl.BlockSpec((r, RET_QK), lambda b, n: (row(b, n), q_blk)),
            pl.BlockSpec((r, RET_QK), lambda b, n: (row(b, n), q_blk + 1)),
            pl.BlockSpec((r, RET_V), lambda b, n: (row(b, n), v_blk)),
            pl.BlockSpec((r, RET_V), lambda b, n: (row(b, n), v_blk + 1)),
            pl.BlockSpec((r, 1), lambda b, n: (row(b, n), 0)),
            pl.BlockSpec((1, RET_DK), const2),
            pl.BlockSpec((1, RET_DK), const2),
            pl.BlockSpec((RET_HEADS, r, r), const3),
            pl.BlockSpec((RET_HEADS, r, RET_DK), const3),
            pl.BlockSpec((RET_HEADS, r, RET_DK), const3),
            pl.BlockSpec((RET_HEADS, 1, RET_DV), const3),
            pl.BlockSpec((1, RET_V), const2),
            pl.BlockSpec((1, GM_DIM), const2),
            pl.BlockSpec((GM_GROUPS, GM_BLOCK, GM_BLOCK), const3),
            pl.BlockSpec((GM_GROUPS, GM_BLOCK, GM_GROUP_DIM), const3),
        ],
        out_specs=(
            pl.BlockSpec((r, GM_DIM), lambda b, n: (row(b, n), 0)),
            pl.BlockSpec((r, RET_V), lambda b, n: (row(b, n), 0)),
        ),
        scratch_shapes=[pltpu.VMEM((RET_HEADS, RET_DK, RET_DV), F32)],
        compiler_params=_params("arbitrary", "arbitrary"),
        name="mixer",
    )(z_all, z_all, z_all, z_all, z_all, pos, freq2, sign2, mask, qdec, kdec, cdec, gn, vgain, ws, gb)


def _outproj_kernel(ya_ref, yb_ref, ga_ref, gb_ref, x_ref, wa_ref, wb_ref, wo_ref,
                    g1_ref, n2_ref, sc2_ref, sh2_ref, x2_ref, h2_ref):
    ya = _dot(ya_ref[...], wa_ref[...])
    yb = _dot(yb_ref[...], wb_ref[...])
    m = ga_ref[...].astype(F32) * ya + gb_ref[...].astype(F32) * yb
    mix = _dot(m.astype(BF16), wo_ref[...])
    x2 = x_ref[...] + g1_ref[...] * mix
    x2_ref[...] = x2
    h2 = (_rms(x2) * n2_ref[...]) * (1.0 + sc2_ref[...]) + sh2_ref[...]
    h2_ref[...] = h2.astype(BF16)


def _outproj(ya_in, yb_in, z_all, x2d, wa, wb, wo, g1, n2, sc2, sh2, seq):
    t, d = x2d.shape
    tm = OUTPROJ_TM
    per_b = seq // tm
    gate_blk = (2 * GM_DIM + 2 * RET_QK + 2 * RET_V) // d
    bvec = pl.BlockSpec((None, 1, d), lambda i: (i // per_b, 0, 0))
    full = lambda a: pl.BlockSpec(a.shape, lambda i: (0, 0))
    return pl.pallas_call(
        _outproj_kernel,
        out_shape=(jax.ShapeDtypeStruct((t, d), F32), jax.ShapeDtypeStruct((t, d), BF16)),
        grid=(t // tm,),
        in_specs=[
            pl.BlockSpec((tm, GM_DIM), lambda i: (i, 0)),
            pl.BlockSpec((tm, RET_V), lambda i: (i, 0)),
            pl.BlockSpec((tm, d), lambda i: (i, gate_blk)),
            pl.BlockSpec((tm, d), lambda i: (i, gate_blk + 1)),
            pl.BlockSpec((tm, d), lambda i: (i, 0)),
            full(wa), full(wb), full(wo),
            bvec, pl.BlockSpec((1, d), lambda i: (0, 0)), bvec, bvec,
        ],
        out_specs=(pl.BlockSpec((tm, d), lambda i: (i, 0)), pl.BlockSpec((tm, d), lambda i: (i, 0))),
        compiler_params=_params("arbitrary"),
        name="out_proj",
    )(ya_in, yb_in, z_all, z_all, x2d, wa, wb, wo, g1, n2, sc2, sh2)


def _extract_top(src_ref, out_ref, n):
    def body(k, _):
        x = src_ref[...]
        m = jnp.max(x, axis=0, keepdims=True)
        out_ref[pl.ds(k, 1), :] = m
        src_ref[...] = jnp.where(x == m, NEG_BIG, x)
        return m

    return lax.fori_loop(0, n, body, jnp.zeros((1, src_ref.shape[1]), F32))


def _peerq_kernel(h2_ref, wq_ref, sk_ref, cnt_ref, e1_ref, rk_ref, e2_ref,
                  wk_ref, v1_ref, v2_ref, cand_ref, tmp_ref):
    k = PEER_TOPK
    q = _dot(h2_ref[...], wq_ref[...])
    for h in range(PEER_HEADS):
        st = []
        for p in range(2):
            hp = 2 * h + p
            qs = q[:, hp * PEER_DHALF:(hp + 1) * PEER_DHALF]
            st.append(_dot_nt(sk_ref[hp], qs, precision=lax.Precision.HIGHEST))
        s1, s2 = st
        wk_ref[...] = s1
        _extract_top(wk_ref, v1_ref, k)
        wk_ref[...] = s2
        _extract_top(wk_ref, v2_ref, k)
        v1 = v1_ref[...]
        v2 = v2_ref[...]
        parts = [v1[0:1] + v2]
        parts += [v1[a:a + 1] + v2[0:8] for a in range(1, 8)]
        parts += [v1[8:16] + v2[0:1]]
        cand = jnp.concatenate(parts, axis=0)
        cand_ref[...] = cand
        tau = _extract_top(cand_ref, tmp_ref, k)
        m1 = v1[0:1]
        m2 = v2[0:1]
        z = jnp.sum(jnp.where(cand >= tau, jnp.exp(cand - (m1 + m2)), 0.0), axis=0, keepdims=True)
        cnt = jnp.zeros_like(s1)
        rk = jnp.zeros_like(s2)
        for b in range(k):
            vb = v2[b:b + 1]
            cnt = cnt + jnp.where(s1 + vb >= tau, 1.0, 0.0)
            rk = rk + jnp.where(vb > s2, 1.0, 0.0)
        cnt_ref[h] = cnt
        rk_ref[h] = rk
        e1_ref[h] = jnp.exp(s1 - m1) / z
        e2_ref[h] = jnp.exp(s2 - m2)


def _peerq(h2, wq, sk):
    t, d = h2.shape
    tq = PEERQ_TQ
    hshape = jax.ShapeDtypeStruct((PEER_HEADS, PEER_NKEYS, t), F32)
    hspec = pl.BlockSpec((PEER_HEADS, PEER_NKEYS, tq), lambda i: (0, 0, i))
    return pl.pallas_call(
        _peerq_kernel,
        out_shape=(hshape, hshape, hshape, hshape),
        grid=(t // tq,),
        in_specs=[
            pl.BlockSpec((tq, d), lambda i: (i, 0)),
            pl.BlockSpec(wq.shape, lambda i: (0, 0)),
            pl.BlockSpec(sk.shape, lambda i: (0, 0, 0)),
        ],
        out_specs=(hspec, hspec, hspec, hspec),
        scratch_shapes=[
            pltpu.VMEM((PEER_NKEYS, tq), F32),
            pltpu.VMEM((PEER_TOPK, tq), F32),
            pltpu.VMEM((PEER_TOPK, tq), F32),
            pltpu.VMEM((80, tq), F32),
            pltpu.VMEM((PEER_TOPK, tq), F32),
        ],
        compiler_params=_params("arbitrary"),
        name="peer_query",
    )(h2, wq, sk)


def _peer_dense_kernel(h2_ref, u_ref, vt_ref, cnt_ref, e1_ref, rk_ref, e2_ref, y_ref,
                       acc_ref, at_ref, p_ref, *, te, tt):
    j = pl.program_id(1)

    @pl.when(j == 0)
    def _():
        acc_ref[...] = jnp.zeros_like(acc_ref)

    at_ref[...] = _dot_nt(u_ref[...], h2_ref[...])

    for c in range(te // PEER_NKEYS):
        rs = slice(c * PEER_NKEYS, (c + 1) * PEER_NKEYS)
        for lb in range(tt // LANES):
            ls = slice(lb * LANES, (lb + 1) * LANES)
            w = jnp.zeros((PEER_NKEYS, LANES), F32)
            for h in range(PEER_HEADS):
                cn = cnt_ref[h, c:c + 1, ls]
                e1 = e1_ref[h, c:c + 1, ls]
                w = w + jnp.where(rk_ref[h, :, ls] < cn, e2_ref[h, :, ls], 0.0) * e1
            p_ref[rs, ls] = (_gelu(at_ref[rs, ls]) * w).astype(BF16)
    acc_ref[...] += _dot(vt_ref[...], p_ref[...])

    @pl.when(j == pl.num_programs(1) - 1)
    def _():
        y_ref[...] = acc_ref[...].T


def _peer_dense(h2, u_bf, vt_bf, cnt, e1, rk, e2):
    t, d = h2.shape
    ne = u_bf.shape[0]
    tt, te = PEER_TT, PEER_TE
    rows = te // PEER_NKEYS
    ispec = pl.BlockSpec((PEER_HEADS, rows, tt), lambda i, j: (0, j, i))
    jspec = pl.BlockSpec((PEER_HEADS, PEER_NKEYS, tt), lambda i, j: (0, 0, i))
    return pl.pallas_call(
        functools.partial(_peer_dense_kernel, te=te, tt=tt),
        out_shape=jax.ShapeDtypeStruct((t, d), F32),
        grid=(t // tt, ne // te),
        in_specs=[
            pl.BlockSpec((tt, d), lambda i, j: (i, 0)),
            pl.BlockSpec((te, d), lambda i, j: (j, 0)),
            pl.BlockSpec((d, te), lambda i, j: (0, j)),
            ispec, ispec, jspec, jspec,
        ],
        out_specs=pl.BlockSpec((tt, d), lambda i, j: (i, 0)),
        scratch_shapes=[
            pltpu.VMEM((d, tt), F32),
            pltpu.VMEM((te, tt), F32),
            pltpu.VMEM((te, tt), BF16),
        ],
        compiler_params=_params("arbitrary", "arbitrary"),
        name="peer_dense",
    )(h2, u_bf, vt_bf, cnt, e1, rk, e2)


def _final_kernel(x2_ref, y_ref, g2_ref, gf_ref, o_ref):
    x3 = x2_ref[...] + g2_ref[...] * y_ref[...]
    o_ref[...] = _rms(x3) * gf_ref[...]


def _final(x2, y, g2, gf, seq):
    t, d = x2.shape
    tm = FINAL_TM
    per_b = seq // tm
    tile = pl.BlockSpec((tm, d), lambda i: (i, 0))
    return pl.pallas_call(
        _final_kernel,
        out_shape=jax.ShapeDtypeStruct((t, d), F32),
        grid=(t // tm,),
        in_specs=[tile, tile, pl.BlockSpec((None, 1, d), lambda i: (i // per_b, 0, 0)),
                  pl.BlockSpec((1, d), lambda i: (0, 0))],
        out_specs=tile,
        compiler_params=_params("arbitrary"),
        name="final",
    )(x2, y, g2, gf)


def _retention_tables(rows):
    log_g = jnp.log1p(-jnp.exp2(-5.0 - jnp.arange(RET_HEADS, dtype=F32)))
    idx = jnp.arange(rows, dtype=F32)
    cid = jnp.arange(rows) // CHUNK
    visible = cid[:, None] >= cid[None, :]
    dist = jnp.abs(idx[:, None] - idx[None, :])
    mask = jnp.where(visible[None], jnp.exp(log_g[:, None, None] * dist[None]), 0.0)
    qdec = jnp.exp(log_g[:, None] * (idx + 1.0))
    kdec = jnp.exp(log_g[:, None] * (rows - 1.0 - idx))
    cdec = jnp.exp(log_g * rows)
    qdec = jnp.broadcast_to(qdec[:, :, None], (RET_HEADS, rows, RET_DK))
    kdec = jnp.broadcast_to(kdec[:, :, None], (RET_HEADS, rows, RET_DK))
    cdec = jnp.broadcast_to(cdec[:, None, None], (RET_HEADS, 1, RET_DV))
    return mask, qdec, kdec, cdec


def _layer(x2d, ada, positions, norm1_g, w_in, w_bg, b_bg, gm_v_g, gm_ws, gm_b, ret_gn_g,
           w_a_out, w_b_out, w_o, norm2_g, peer_wq, peer_subkeys, peer_u, peer_v, batch, seq):
    t, d = x2d.shape
    sh1, sc1, ga1, sh2, sc2, ga2 = [a.reshape(batch, 1, d) for a in jnp.split(ada, 6, axis=-1)]

    w_cat = jnp.concatenate([w_in, w_bg], axis=1).astype(BF16)
    b_cat = jnp.concatenate([jnp.zeros((w_in.shape[1],), F32), b_bg]).reshape(1, -1)
    z_all = _inproj(x2d, norm1_g.reshape(1, d), sc1, sh1, w_cat, b_cat, seq)

    half = RET_DK // 2
    freqs = ROPE_BASE ** (-jnp.arange(half, dtype=F32) / half)
    freq2 = jnp.concatenate([freqs, freqs]).reshape(1, RET_DK)
    sign2 = jnp.concatenate([-jnp.ones((half,), F32), jnp.ones((half,), F32)]).reshape(1, RET_DK)
    pos = positions.astype(F32).reshape(t, 1)
    mask, qdec, kdec, cdec = _retention_tables(MIX_ROWS)
    gb = jnp.broadcast_to(gm_b[:, :, None], (GM_GROUPS, GM_BLOCK, GM_GROUP_DIM))
    ya_in, yb_in = _mixer(z_all, pos, freq2, sign2, mask, qdec, kdec, cdec,
                          ret_gn_g.reshape(1, RET_V), gm_v_g.reshape(1, GM_DIM), gm_ws, gb, batch, seq)

    x2, h2 = _outproj(ya_in, yb_in, z_all, x2d, w_a_out.astype(BF16), w_b_out.astype(BF16),
                      w_o.astype(BF16), ga1, norm2_g.reshape(1, d), sc2, sh2, seq)

    sk = peer_subkeys.reshape(PEER_HEADS * 2, PEER_NKEYS, PEER_DHALF)
    cnt, e1, rk, e2 = _peerq(h2, peer_wq.astype(BF16), sk)
    y = _peer_dense(h2, peer_u.astype(BF16), peer_v.T.astype(BF16), cnt, e1, rk, e2)
    return x2, y, ga2


def kernel(x, c, positions, w_ada, b_ada, norm1_g, w_in, w_branch_gate, b_branch_gate, gm_v_g, gm_ws, gm_b,
           ret_gn_g, w_a_out, w_b_out, w_o, norm2_g, peer_wq, peer_subkeys, peer_u, peer_v, norm_f_g):
    batch, seq, d = x.shape
    assert w_ada.shape[0] == 1, "single-layer block only"
    x2d = x.reshape(batch * seq, d)
    c8 = jnp.pad(c, ((0, 8 - batch), (0, 0)))
    ada = _ada(c8, w_ada[0], b_ada[0].reshape(1, -1))[:batch]
    x2, y, ga2 = _layer(x2d, ada, positions, norm1_g[0], w_in[0], w_branch_gate[0], b_branch_gate[0],
                        gm_v_g[0], gm_ws[0], gm_b[0], ret_gn_g[0], w_a_out[0], w_b_out[0], w_o[0],
                        norm2_g[0], peer_wq[0], peer_subkeys[0], peer_u[0], peer_v[0], batch, seq)
    out = _final(x2, y, ga2, norm_f_g.reshape(1, d), seq)
    return out.reshape(batch, seq, d)
```
